```python
import jax, jax.numpy as jnp
from jax import lax
import numpy as np

D_MODEL = 1024
BATCH = 32
SEQ = 2048
DEPTH = 4

CONF_WIDTH = D_MODEL // 4
CONF_KERNEL = 31
SC_WIDTH = D_MODEL // 4
SC_KERNEL = 3
ATTN_WIDTH = D_MODEL // 2
ATTN_HEADS = 4
V_HEAD_DIM = ATTN_WIDTH // ATTN_HEADS
QK_HEAD_DIM = V_HEAD_DIM // 2
D_MIX = CONF_WIDTH + SC_WIDTH + ATTN_WIDTH
QK_WIDTH = ATTN_HEADS * 2 * QK_HEAD_DIM
IN_WIDTH = 2 * CONF_WIDTH + 3 * SC_WIDTH + 2 * QK_WIDTH + ATTN_WIDTH
D_FF = -(-8 * D_MODEL // (3 * 256)) * 256
ROPE_THETA = 10000.0
Q_BLOCK = 128
EPS = 1e-6

kernel_name = "hybrid_conformer_shortconv_diffattn_encoder"


def rms_norm(x, g):
    xf = x.astype(jnp.float32)
    y = xf * lax.rsqrt(jnp.mean(xf * xf, axis=-1, keepdims=True) + EPS)
    return (y * g.astype(jnp.float32)).astype(x.dtype)


def modulate(h, shift, scale):
    return h * (1.0 + scale[:, None, :]) + shift[:, None, :]


def depthwise_conv(u, w, b=None):
    k = w.shape[0]
    out = lax.conv_general_dilated(
        u, w[:, None, :].astype(u.dtype), window_strides=(1,),
        padding=[(k // 2, k // 2)], dimension_numbers=("NWC", "WIO", "NWC"),
        feature_group_count=u.shape[-1])
    if b is not None:
        out = out + b.astype(out.dtype)
    return out


def rope_tables(positions):
    inv_freq = ROPE_THETA ** (-jnp.arange(0, QK_HEAD_DIM, 2, dtype=jnp.float32) / QK_HEAD_DIM)
    ang = positions.astype(jnp.float32)[:, None] * inv_freq[None, :]
    ang = jnp.concatenate([ang, ang], axis=-1)
    return jnp.cos(ang), jnp.sin(ang)


def apply_rope(t, cos, sin):
    half = QK_HEAD_DIM // 2
    rot = jnp.concatenate([-t[..., half:], t[..., :half]], axis=-1)
    return t * cos[None, :, None, None, :] + rot * sin[None, :, None, None, :]


def conformer_conv(za, conv_w, conv_b, norm_g):
    val, gate = jnp.split(za, 2, axis=-1)
    u = val * jax.nn.sigmoid(gate)
    u = depthwise_conv(u, conv_w, conv_b)
    return jax.nn.silu(rms_norm(u, norm_g))


def short_gated_conv(zb, conv_w, norm_g):
    bg, cg, hv = jnp.split(zb, 3, axis=-1)
    y = bg * depthwise_conv(cg * hv, conv_w)
    return rms_norm(y, norm_g)


def diff_attention(zq, zk, zv, cos, sin, q_g, k_g, lq1, lk1, lq2, lk2, out_g, lambda_init):
    bsz, seq = zq.shape[0], zq.shape[1]
    q = zq.reshape(bsz, seq, ATTN_HEADS, 2, QK_HEAD_DIM)
    k = zk.reshape(bsz, seq, ATTN_HEADS, 2, QK_HEAD_DIM)
    v = zv.reshape(bsz, seq, ATTN_HEADS, V_HEAD_DIM)
    q = apply_rope(rms_norm(q, q_g), cos, sin) * (QK_HEAD_DIM ** -0.5)
    k = apply_rope(rms_norm(k, k_g), cos, sin)
    lam = (jnp.exp(jnp.sum(lq1.astype(jnp.float32) * lk1.astype(jnp.float32)))
           - jnp.exp(jnp.sum(lq2.astype(jnp.float32) * lk2.astype(jnp.float32)))
           + lambda_init)
    n_blk = seq // Q_BLOCK
    q_blocks = jnp.moveaxis(q.reshape(bsz, n_blk, Q_BLOCK, ATTN_HEADS, 2, QK_HEAD_DIM), 1, 0)

    def block(qb):
        s = jnp.einsum("bqhcd,bkhcd->bhcqk", qb, k).astype(jnp.float32)
        p = jax.nn.softmax(s, axis=-1)
        w = p[:, :, 0] - lam * p[:, :, 1]
        return jnp.einsum("bhqk,bkhe->bqhe", w, v.astype(jnp.float32))

    o = lax.map(block, q_blocks)
    o = jnp.moveaxis(o, 0, 1).reshape(bsz, seq, ATTN_HEADS, V_HEAD_DIM).astype(zv.dtype)
    o = rms_norm(o, out_g) * (1.0 - lambda_init)
    return o.reshape(bsz, seq, ATTN_WIDTH)


def swiglu(h, w_gate_up, w_down):
    g, u = jnp.split(h @ w_gate_up, 2, axis=-1)
    return (jax.nn.silu(g) * u) @ w_down


def setup_inputs(seed: int = 0) -> dict:
    key = jax.random.key(seed)
    ks = jax.random.split(key, 24)
    f32 = jnp.float32
    nrm = lambda k, shape, s: jax.random.normal(k, shape, f32) * s
    gain = lambda k, shape: 1.0 + 0.02 * jax.random.normal(k, shape, f32)
    L, D = DEPTH, D_MODEL
    return {
        "x": nrm(ks[0], (BATCH, SEQ, D), 1.0),
        "c": nrm(ks[1], (BATCH, D), 1.0),
        "positions": jnp.arange(SEQ, dtype=jnp.int32),
        "norm1_g": gain(ks[2], (L, D)),
        "norm2_g": gain(ks[3], (L, D)),
        "w_ada": nrm(ks[4], (L, D, 6 * D), D ** -0.5),
        "b_ada": nrm(ks[5], (L, 6 * D), 0.02),
        "w_in": nrm(ks[6], (L, D, IN_WIDTH), D ** -0.5),
        "conv_a_w": nrm(ks[7], (L, CONF_KERNEL, CONF_WIDTH), CONF_KERNEL ** -0.5),
        "conv_a_b": nrm(ks[8], (L, CONF_WIDTH), 0.02),
        "conv_a_norm_g": gain(ks[9], (L, CONF_WIDTH)),
        "conv_b_w": nrm(ks[10], (L, SC_KERNEL, SC_WIDTH), SC_KERNEL ** -0.5),
        "sc_norm_g": gain(ks[11], (L, SC_WIDTH)),
        "q_norm_g": gain(ks[12], (L, QK_HEAD_DIM)),
        "k_norm_g": gain(ks[13], (L, QK_HEAD_DIM)),
        "lam_q1": nrm(ks[14], (L, QK_HEAD_DIM), 0.1),
        "lam_k1": nrm(ks[15], (L, QK_HEAD_DIM), 0.1),
        "lam_q2": nrm(ks[16], (L, QK_HEAD_DIM), 0.1),
        "lam_k2": nrm(ks[17], (L, QK_HEAD_DIM), 0.1),
        "attn_norm_g": gain(ks[18], (L, V_HEAD_DIM)),
        "w_out": nrm(ks[19], (L, D_MIX, D), D_MIX ** -0.5),
        "w_gate_up": nrm(ks[20], (L, D, 2 * D_FF), D ** -0.5),
        "w_down": nrm(ks[21], (L, D_FF, D), D_FF ** -0.5),
    }


def reference(x, c, positions, norm1_g, norm2_g, w_ada, b_ada, w_in, conv_a_w, conv_a_b,
              conv_a_norm_g, conv_b_w, sc_norm_g, q_norm_g, k_norm_g, lam_q1, lam_k1,
              lam_q2, lam_k2, attn_norm_g, w_out, w_gate_up, w_down):
    cos, sin = rope_tables(positions)
    c_act = jax.nn.silu(c)
    split_idx = np.cumsum([2 * CONF_WIDTH, 3 * SC_WIDTH, QK_WIDTH, QK_WIDTH])
    for l in range(DEPTH):
        lambda_init = 0.8 - 0.6 * float(np.exp(-0.3 * l))
        mod = c_act @ w_ada[l] + b_ada[l]
        sh1, sc1, g1, sh2, sc2, g2 = jnp.split(mod, 6, axis=-1)

        h = modulate(rms_norm(x, norm1_g[l]), sh1, sc1)
        z = h @ w_in[l]
        za, zb, zq, zk, zv = jnp.split(z, split_idx, axis=-1)
        ya = conformer_conv(za, conv_a_w[l], conv_a_b[l], conv_a_norm_g[l])
        yb = short_gated_conv(zb, conv_b_w[l], sc_norm_g[l])
        yc = diff_attention(zq, zk, zv, cos, sin, q_norm_g[l], k_norm_g[l], lam_q1[l], lam_k1[l],
                            lam_q2[l], lam_k2[l], attn_norm_g[l], lambda_init)
        mix = jnp.concatenate([ya, yb, yc], axis=-1) @ w_out[l]
        x = x + g1[:, None, :] * mix

        h = modulate(rms_norm(x, norm2_g[l]), sh2, sc2)
        x = x + g2[:, None, :] * swiglu(h, w_gate_up[l], w_down[l])
    return x
```

```python
import functools

import numpy as np
import jax
import jax.numpy as jnp
from jax import lax
from jax.experimental import pallas as pl
from jax.experimental.pallas import tpu as pltpu

D_MODEL = 1024
BATCH = 32
SEQ = 2048
DEPTH = 4
CONF_WIDTH = D_MODEL // 4
CONF_KERNEL = 31
SC_WIDTH = D_MODEL // 4
SC_KERNEL = 3
ATTN_WIDTH = D_MODEL // 2
ATTN_HEADS = 4
V_HEAD_DIM = ATTN_WIDTH // ATTN_HEADS
QK_HEAD_DIM = V_HEAD_DIM // 2
D_MIX = CONF_WIDTH + SC_WIDTH + ATTN_WIDTH
QK_WIDTH = ATTN_HEADS * 2 * QK_HEAD_DIM
IN_WIDTH = 2 * CONF_WIDTH + 3 * SC_WIDTH + 2 * QK_WIDTH + ATTN_WIDTH
D_FF = -(-8 * D_MODEL // (3 * 256)) * 256
ROPE_THETA = 10000.0
EPS = 1e-6

ZB_OFF = 2 * CONF_WIDTH
ZQ_OFF = ZB_OFF + 3 * SC_WIDTH
ZK_OFF = ZQ_OFF + QK_WIDTH
ZV_OFF = ZK_OFF + QK_WIDTH

V7X_LANES = 128
V7X_VMEM_BYTES = 64 * 1024 * 1024

BF16 = jnp.bfloat16
F32 = jnp.float32

IN_TM = 512
FFN_TM = 512
FFN_CK = 256
ATTN_TQ = 256
CONV_ROWS = 64
CONV_PAD = 16


def _cparams(sem, vmem_mib):
    return pltpu.CompilerParams(dimension_semantics=sem, vmem_limit_bytes=vmem_mib * 1024 * 1024)


def _ada_kernel(c_ref, w_ref, b_ref, o_ref):
    c = c_ref[...]
    c_act = (c * jax.nn.sigmoid(c)).astype(BF16)
    o_ref[0] = jnp.dot(c_act, w_ref[0].astype(BF16), preferred_element_type=F32) + b_ref[0]


def _ada_mods(c, w_ada, b_ada):
    nblk = 6
    return pl.pallas_call(
        _ada_kernel,
        grid=(DEPTH, nblk),
        in_specs=[
            pl.BlockSpec((BATCH, D_MODEL), lambda l, j: (0, 0)),
            pl.BlockSpec((1, D_MODEL, D_MODEL), lambda l, j: (l, 0, j)),
            pl.BlockSpec((1, 1, D_MODEL), lambda l, j: (l, 0, j)),
        ],
        out_specs=pl.BlockSpec((1, BATCH, D_MODEL), lambda l, j: (l, 0, j)),
        out_shape=jax.ShapeDtypeStruct((DEPTH, BATCH, 6 * D_MODEL), F32),
        compiler_params=_cparams(("arbitrary", "arbitrary"), 32),
        name="ada_mods",
    )(c, w_ada, b_ada.reshape(DEPTH, 1, 6 * D_MODEL))


def _rope_kernel(pos_ref, freq_ref, sign_ref, cos_ref, sin_ref):
    ang = pos_ref[...].astype(F32) * freq_ref[...]
    cos_ref[...] = jnp.cos(ang)
    sin_ref[...] = jnp.sin(ang) * sign_ref[...]


def _rope_tables(positions):
    inv_freq = ROPE_THETA ** (-np.arange(0, QK_HEAD_DIM, 2, dtype=np.float32) / QK_HEAD_DIM)
    freq = np.tile(inv_freq.astype(np.float32), 4)[None, :]
    sign = np.tile(np.concatenate([-np.ones(32, np.float32), np.ones(32, np.float32)]), 2)[None, :]
    full = lambda: (0, 0)
    return pl.pallas_call(
        _rope_kernel,
        in_specs=[
            pl.BlockSpec((SEQ, 1), full),
            pl.BlockSpec((1, V_HEAD_DIM), full),
            pl.BlockSpec((1, V_HEAD_DIM), full),
        ],
        out_specs=[pl.BlockSpec((SEQ, V_HEAD_DIM), full)] * 2,
        out_shape=[jax.ShapeDtypeStruct((SEQ, V_HEAD_DIM), F32)] * 2,
        name="rope_tables",
    )(positions.reshape(SEQ, 1), jnp.asarray(freq), jnp.asarray(sign))


def _inproj_kernel(x_ref, mod_ref, g_ref, w_ref, z_ref):
    x = x_ref[0]
    shift = mod_ref[0, :, 0:D_MODEL]
    scale = mod_ref[0, :, D_MODEL:2 * D_MODEL]
    gain = g_ref[...] * (1.0 + scale)
    ms = jnp.mean(x * x, axis=-1, keepdims=True)
    h = x * lax.rsqrt(ms + EPS) * gain + shift
    z = jnp.dot(h.astype(BF16), w_ref[...], preferred_element_type=F32)
    z_ref[0] = z.astype(BF16)


def _in_projection(x, mods_l, norm_g, w_in_bf16):
    tm = IN_TM
    return pl.pallas_call(
        _inproj_kernel,
        grid=(BATCH, SEQ // tm),
        in_specs=[
            pl.BlockSpec((1, tm, D_MODEL), lambda b, i: (b, i, 0)),
            pl.BlockSpec((1, 1, 6 * D_MODEL), lambda b, i: (b, 0, 0)),
            pl.BlockSpec((1, D_MODEL), lambda b, i: (0, 0)),
            pl.BlockSpec((D_MODEL, IN_WIDTH), lambda b, i: (0, 0)),
        ],
        out_specs=pl.BlockSpec((1, tm, IN_WIDTH), lambda b, i: (b, i, 0)),
        out_shape=jax.ShapeDtypeStruct((BATCH, SEQ, IN_WIDTH), BF16),
        compiler_params=_cparams(("arbitrary", "arbitrary"), 48),
        name="in_projection",
    )(x, mods_l.reshape(BATCH, 1, 6 * D_MODEL), norm_g.reshape(1, D_MODEL), w_in_bf16)


def _rms_rows(y, g):
    ms = jnp.mean(y * y, axis=-1, keepdims=True)
    return y * lax.rsqrt(ms + EPS) * g


def _conv_kernel(val_ref, gate_ref, bg_ref, cg_ref, hv_ref, wa_ref, ba_ref, ga_ref, wb_ref, gb_ref,
                 o_ref, upad_ref, vpad_ref, uwin_ref, vwin_ref):
    zeros = jnp.zeros((CONV_PAD, CONF_WIDTH), F32)
    upad_ref[0:CONV_PAD, :] = zeros
    upad_ref[CONV_PAD + SEQ:CONV_PAD + SEQ + CONV_PAD, :] = zeros
    vpad_ref[0:CONV_PAD, :] = zeros
    vpad_ref[CONV_PAD + SEQ:CONV_PAD + SEQ + CONV_PAD, :] = zeros
    upad_ref[CONV_PAD:CONV_PAD + SEQ, :] = val_ref[0].astype(F32) * jax.nn.sigmoid(gate_ref[0].astype(F32))
    vpad_ref[CONV_PAD:CONV_PAD + SEQ, :] = cg_ref[0].astype(F32) * hv_ref[0].astype(F32)

    rows = CONV_ROWS
    half_a = CONF_KERNEL // 2
    half_b = SC_KERNEL // 2

    def fill(i, carry):
        r0 = pl.multiple_of(i * rows, rows)
        uwin_ref[i] = upad_ref[pl.ds(r0, rows + 2 * CONV_PAD), :]
        vwin_ref[i] = vpad_ref[pl.ds(r0, rows + 2 * CONV_PAD), :]
        return carry

    lax.fori_loop(0, SEQ // rows, fill, 0)

    def body(i, carry):
        r0 = pl.multiple_of(i * rows, rows)
        acc = jnp.broadcast_to(ba_ref[...], (rows, CONF_WIDTH))
        for t in range(CONF_KERNEL):
            acc = acc + wa_ref[t:t + 1, :] * uwin_ref[i, CONV_PAD - half_a + t:CONV_PAD - half_a + t + rows, :]
        ya = _rms_rows(acc, ga_ref[...])
        ya = ya * jax.nn.sigmoid(ya)
        o_ref[0, pl.ds(r0, rows), 0:CONF_WIDTH] = ya.astype(BF16)

        accb = jnp.zeros((rows, SC_WIDTH), F32)
        for t in range(SC_KERNEL):
            accb = accb + wb_ref[t:t + 1, :] * vwin_ref[i, CONV_PAD - half_b + t:CONV_PAD - half_b + t + rows, :]
        yb = bg_ref[0, pl.ds(r0, rows), :].astype(F32) * accb
        o_ref[0, pl.ds(r0, rows), CONF_WIDTH:CONF_WIDTH + SC_WIDTH] = _rms_rows(yb, gb_ref[...]).astype(BF16)
        return carry

    lax.fori_loop(0, SEQ // rows, body, 0)


def _conv_branches(z, conv_a_w, conv_a_b, conv_a_norm_g, conv_b_w, sc_norm_g):
    w = CONF_WIDTH
    zcol = lambda j: pl.BlockSpec((1, SEQ, w), lambda b, j=j: (b, 0, j))
    full2 = lambda shape: pl.BlockSpec(shape, lambda b: (0, 0))
    return pl.pallas_call(
        _conv_kernel,
        grid=(BATCH,),
        in_specs=[
            zcol(0), zcol(1),
            zcol(ZB_OFF // w), zcol(ZB_OFF // w + 1), zcol(ZB_OFF // w + 2),
            full2((CONF_KERNEL, w)), full2((1, w)), full2((1, w)),
            full2((SC_KERNEL, w)), full2((1, w)),
        ],
        out_specs=pl.BlockSpec((1, SEQ, 2 * w), lambda b: (b, 0, 0)),
        out_shape=jax.ShapeDtypeStruct((BATCH, SEQ, 2 * w), BF16),
        scratch_shapes=[
            pltpu.VMEM((SEQ + 2 * CONV_PAD, w), F32),
            pltpu.VMEM((SEQ + 2 * CONV_PAD, w), F32),
            pltpu.VMEM((SEQ // CONV_ROWS, CONV_ROWS + 2 * CONV_PAD, w), F32),
            pltpu.VMEM((SEQ // CONV_ROWS, CONV_ROWS + 2 * CONV_PAD, w), F32),
        ],
        compiler_params=_cparams(("arbitrary",), 48),
        name="conv_branches",
    )(z, z, z, z, z, conv_a_w, conv_a_b.reshape(1, w), conv_a_norm_g.reshape(1, w),
      conv_b_w, sc_norm_g.reshape(1, w))


def _attn_kernel(zq_ref, zk_ref, zv_ref, cos_ref, sin_ref, qg_ref, kg_ref, lam_ref, og_ref, seg_ref,
                 o_ref, qh_ref, k0_ref, k1_ref, vt_ref, *, lambda_init):
    cos = cos_ref[...]
    sin = sin_ref[...]
    seg = seg_ref[...]
    lane = lax.broadcasted_iota(jnp.int32, (SEQ, V_HEAD_DIM), 1)
    low_quarter = (lane % QK_HEAD_DIM) < (QK_HEAD_DIM // 2)
    in_half0 = lane < QK_HEAD_DIM

    def norm_rope(z_ref, g_ref):
        t = z_ref[0].astype(F32)
        tt = t * t
        tt_hi = tt.astype(BF16)
        tt_lo = (tt - tt_hi.astype(F32)).astype(BF16)
        ms = (jnp.dot(tt_hi, seg, preferred_element_type=F32)
              + jnp.dot(tt_lo, seg, preferred_element_type=F32))
        y = t * lax.rsqrt(ms + EPS) * g_ref[...]
        swapped = jnp.where(low_quarter,
                            pltpu.roll(y, V_HEAD_DIM - QK_HEAD_DIM // 2, 1),
                            pltpu.roll(y, QK_HEAD_DIM // 2, 1))
        return y * cos + swapped * sin

    qh_ref[...] = (norm_rope(zq_ref, qg_ref) * (QK_HEAD_DIM ** -0.5)).astype(BF16)
    kh = norm_rope(zk_ref, kg_ref)
    k0_ref[...] = jnp.where(in_half0, kh, 0.0).astype(BF16)
    k1_ref[...] = jnp.where(in_half0, 0.0, kh).astype(BF16)
    vt_ref[...] = zv_ref[0].astype(F32).T.astype(BF16)

    lam4 = lam_ref[...]
    lam = (jnp.exp(jnp.sum(lam4[0:1] * lam4[1:2], axis=-1, keepdims=True))
           - jnp.exp(jnp.sum(lam4[2:3] * lam4[3:4], axis=-1, keepdims=True))
           + lambda_init)
    out_gain = og_ref[...] * (1.0 - lambda_init)

    tq = ATTN_TQ
    nt = (((1,), (1,)), ((), ()))

    def softmax_cols(s):
        m = jnp.max(s, axis=0, keepdims=True)
        p = jnp.exp(s - m)
        return p.astype(BF16), jnp.sum(p, axis=0, keepdims=True)

    def body(i, carry):
        r0 = pl.multiple_of(i * tq, tq)
        qb = qh_ref[pl.ds(r0, tq), :]
        p0, l0 = softmax_cols(lax.dot_general(k0_ref[...], qb, nt, preferred_element_type=F32))
        p1, l1 = softmax_cols(lax.dot_general(k1_ref[...], qb, nt, preferred_element_type=F32))
        o0 = jnp.dot(vt_ref[...], p0, preferred_element_type=F32)
        o1 = jnp.dot(vt_ref[...], p1, preferred_element_type=F32)
        o = o0 / l0 - o1 * (lam / l1)
        ms = jnp.mean(o * o, axis=0, keepdims=True)
        y = o * lax.rsqrt(ms + EPS) * out_gain
        o_ref[0, pl.ds(r0, tq), :] = y.T.astype(BF16)
        return carry

    lax.fori_loop(0, SEQ // tq, body, 0)


def _diff_attention(z, cos, sin, q_g, k_g, lam4, out_g, lambda_init):
    hd = V_HEAD_DIM
    seg = np.zeros((hd, hd), np.float32)
    seg[:QK_HEAD_DIM, :QK_HEAD_DIM] = 1.0 / QK_HEAD_DIM
    seg[QK_HEAD_DIM:, QK_HEAD_DIM:] = 1.0 / QK_HEAD_DIM
    zcol = lambda off: pl.BlockSpec((1, SEQ, hd), lambda b, h, off=off: (b, 0, off // hd + h))
    full2 = lambda shape: pl.BlockSpec(shape, lambda b, h: (0, 0))
    return pl.pallas_call(
        functools.partial(_attn_kernel, lambda_init=lambda_init),
        grid=(BATCH, ATTN_HEADS),
        in_specs=[
            zcol(ZQ_OFF), zcol(ZK_OFF), zcol(ZV_OFF),
            full2((SEQ, hd)), full2((SEQ, hd)),
            full2((1, hd)), full2((1, hd)),
            full2((4, QK_HEAD_DIM)), full2((hd, 1)), full2((hd, hd)),
        ],
        out_specs=pl.BlockSpec((1, SEQ, hd), lambda b, h: (b, 0, h)),
        out_shape=jax.ShapeDtypeStruct((BATCH, SEQ, ATTN_WIDTH), BF16),
        scratch_shapes=[
            pltpu.VMEM((SEQ, hd), BF16),
            pltpu.VMEM((SEQ, hd), BF16),
            pltpu.VMEM((SEQ, hd), BF16),
            pltpu.VMEM((hd, SEQ), BF16),
        ],
        compiler_params=_cparams(("arbitrary", "arbitrary"), 48),
        name="diff_attention",
    )(z, z, z, cos, sin, jnp.tile(q_g, 2).reshape(1, hd), jnp.tile(k_g, 2).reshape(1, hd),
      lam4, out_g.reshape(hd, 1), jnp.asarray(seg, BF16))


def _ffn_kernel(x_ref, yab_ref, yc_ref, mod_ref, g_ref, wo_ref, wgu_ref, wd_ref, o_ref):
    d = D_MODEL
    gate1 = mod_ref[0, :, 2 * d:3 * d]
    shift2 = mod_ref[0, :, 3 * d:4 * d]
    scale2 = mod_ref[0, :, 4 * d:5 * d]
    gate2 = mod_ref[0, :, 5 * d:6 * d]
    n_ab = CONF_WIDTH + SC_WIDTH
    mix = (jnp.dot(yab_ref[0], wo_ref[0:n_ab, :], preferred_element_type=F32)
           + jnp.dot(yc_ref[0], wo_ref[n_ab:D_MIX, :], preferred_element_type=F32))
    x1 = x_ref[0] + gate1 * mix
    ms = jnp.mean(x1 * x1, axis=-1, keepdims=True)
    h = (x1 * lax.rsqrt(ms + EPS) * (g_ref[...] * (1.0 + scale2)) + shift2).astype(BF16)
    acc = jnp.zeros(x1.shape, F32)
    for j in range(D_FF // FFN_CK):
        c0 = j * FFN_CK
        g = jnp.dot(h, wgu_ref[:, c0:c0 + FFN_CK], preferred_element_type=F32)
        u = jnp.dot(h, wgu_ref[:, D_FF + c0:D_FF + c0 + FFN_CK], preferred_element_type=F32)
        a = (g * jax.nn.sigmoid(g) * u).astype(BF16)
        acc = acc + jnp.dot(a, wd_ref[c0:c0 + FFN_CK, :], preferred_element_type=F32)
    o_ref[0] = x1 + gate2 * acc


def _out_ffn(x, yab, yc, mods_l, norm_g, w_out, w_gate_up, w_down):
    tm = FFN_TM
    resident = lambda shape: pl.BlockSpec(shape, lambda b, i: (0, 0), pipeline_mode=pl.Buffered(1))
    tok = lambda width: pl.BlockSpec((1, tm, width), lambda b, i: (b, i, 0))
    return pl.pallas_call(
        _ffn_kernel,
        grid=(BATCH, SEQ // tm),
        in_specs=[
            tok(D_MODEL), tok(CONF_WIDTH + SC_WIDTH), tok(ATTN_WIDTH),
            pl.BlockSpec((1, 1, 6 * D_MODEL), lambda b, i: (b, 0, 0)),
            pl.BlockSpec((1, D_MODEL), lambda b, i: (0, 0)),
            resident((D_MIX, D_MODEL)), resident((D_MODEL, 2 * D_FF)), resident((D_FF, D_MODEL)),
        ],
        out_specs=tok(D_MODEL),
        out_shape=jax.ShapeDtypeStruct((BATCH, SEQ, D_MODEL), F32),
        compiler_params=_cparams(("arbitrary", "arbitrary"), 56),
        name="out_ffn",
    )(x, yab, yc, mods_l.reshape(BATCH, 1, 6 * D_MODEL), norm_g.reshape(1, D_MODEL),
      w_out, w_gate_up, w_down)


def kernel(x, c, positions, norm1_g, norm2_g, w_ada, b_ada, w_in, conv_a_w, conv_a_b, conv_a_norm_g,
           conv_b_w, sc_norm_g, q_norm_g, k_norm_g, lam_q1, lam_k1, lam_q2, lam_k2, attn_norm_g,
           w_out, w_gate_up, w_down):
    mods = _ada_mods(c, w_ada, b_ada)
    cos, sin = _rope_tables(positions)
    w_in_b = w_in.astype(BF16)
    w_out_b = w_out.astype(BF16)
    w_gu_b = w_gate_up.astype(BF16)
    w_down_b = w_down.astype(BF16)
    for l in range(DEPTH):
        lambda_init = 0.8 - 0.6 * float(np.exp(-0.3 * l))
        z = _in_projection(x, mods[l], norm1_g[l], w_in_b[l])
        yab = _conv_branches(z, conv_a_w[l], conv_a_b[l], conv_a_norm_g[l], conv_b_w[l], sc_norm_g[l])
        lam4 = jnp.stack([lam_q1[l], lam_k1[l], lam_q2[l], lam_k2[l]])
        yc = _diff_attention(z, cos, sin, q_norm_g[l], k_norm_g[l], lam4, attn_norm_g[l], lambda_init)
        x = _out_ffn(x, yab, yc, mods[l], norm2_g[l], w_out_b[l], w_gu_b[l], w_down_b[l])
    return x
```

```python
import functools

import numpy as np
import jax
import jax.numpy as jnp
from jax import lax
from jax.experimental import pallas as pl
from jax.experimental.pallas import tpu as pltpu

D_MODEL = 1024
BATCH = 32
SEQ = 2048
DEPTH = 4
CONF_WIDTH = D_MODEL // 4
CONF_KERNEL = 31
SC_WIDTH = D_MODEL // 4
SC_KERNEL = 3
ATTN_WIDTH = D_MODEL // 2
ATTN_HEADS = 4
V_HEAD_DIM = ATTN_WIDTH // ATTN_HEADS
QK_HEAD_DIM = V_HEAD_DIM // 2
D_MIX = CONF_WIDTH + SC_WIDTH + ATTN_WIDTH
QK_WIDTH = ATTN_HEADS * 2 * QK_HEAD_DIM
IN_WIDTH = 2 * CONF_WIDTH + 3 * SC_WIDTH + 2 * QK_WIDTH + ATTN_WIDTH
D_FF = -(-8 * D_MODEL // (3 * 256)) * 256
ROPE_THETA = 10000.0
EPS = 1e-6
LOG2_E = 1.4426950408889634

ZB_OFF = 2 * CONF_WIDTH
ZQ_OFF = ZB_OFF + 3 * SC_WIDTH
ZK_OFF = ZQ_OFF + QK_WIDTH
ZV_OFF = ZK_OFF + QK_WIDTH

V7X_LANES = 128
V7X_VMEM_BYTES = 64 * 1024 * 1024

BF16 = jnp.bfloat16
F32 = jnp.float32

IN_TM = 512
FFN_TM = 512
FFN_CK = 256
ATTN_TQ = 512
ATTN_KC = 256
ATTN_LOOKAHEAD = 2
SAFE_EXP2_SHIFT = 60.0
CONV_ROWS = 64
CONV_PAD = 16


def _cparams(sem, vmem_mib):
    return pltpu.CompilerParams(dimension_semantics=sem, vmem_limit_bytes=vmem_mib * 1024 * 1024)


def _ada_kernel(c_ref, w_ref, b_ref, o_ref):
    c = c_ref[...]
    c_act = (c * jax.nn.sigmoid(c)).astype(BF16)
    o_ref[0] = jnp.dot(c_act, w_ref[0].astype(BF16), preferred_element_type=F32) + b_ref[0]


def _ada_mods(c, w_ada, b_ada):
    nblk = 6
    return pl.pallas_call(
        _ada_kernel,
        grid=(DEPTH, nblk),
        in_specs=[
            pl.BlockSpec((BATCH, D_MODEL), lambda l, j: (0, 0)),
            pl.BlockSpec((1, D_MODEL, D_MODEL), lambda l, j: (l, 0, j)),
            pl.BlockSpec((1, 1, D_MODEL), lambda l, j: (l, 0, j)),
        ],
        out_specs=pl.BlockSpec((1, BATCH, D_MODEL), lambda l, j: (l, 0, j)),
        out_shape=jax.ShapeDtypeStruct((DEPTH, BATCH, 6 * D_MODEL), F32),
        compiler_params=_cparams(("arbitrary", "arbitrary"), 32),
        name="ada_mods",
    )(c, w_ada, b_ada.reshape(DEPTH, 1, 6 * D_MODEL))


def _rope_kernel(pos_ref, freq_ref, sign_ref, cos_ref, sin_ref):
    ang = pos_ref[...].astype(F32) * freq_ref[...]
    cos_ref[...] = jnp.cos(ang)
    sin_ref[...] = jnp.sin(ang) * sign_ref[...]


def _rope_tables(positions):
    inv_freq = ROPE_THETA ** (-np.arange(0, QK_HEAD_DIM, 2, dtype=np.float32) / QK_HEAD_DIM)
    freq = np.tile(inv_freq.astype(np.float32), 4)[None, :]
    sign = np.tile(np.concatenate([-np.ones(32, np.float32), np.ones(32, np.float32)]), 2)[None, :]
    full = lambda: (0, 0)
    return pl.pallas_call(
        _rope_kernel,
        in_specs=[
            pl.BlockSpec((SEQ, 1), full),
            pl.BlockSpec((1, V_HEAD_DIM), full),
            pl.BlockSpec((1, V_HEAD_DIM), full),
        ],
        out_specs=[pl.BlockSpec((SEQ, V_HEAD_DIM), full)] * 2,
        out_shape=[jax.ShapeDtypeStruct((SEQ, V_HEAD_DIM), F32)] * 2,
        name="rope_tables",
    )(positions.reshape(SEQ, 1), jnp.asarray(freq), jnp.asarray(sign))


def _inproj_kernel(x_ref, mod_ref, g_ref, w_ref, z_ref):
    x = x_ref[0]
    shift = mod_ref[0, :, 0:D_MODEL]
    scale = mod_ref[0, :, D_MODEL:2 * D_MODEL]
    gain = g_ref[...] * (1.0 + scale)
    ms = jnp.mean(x * x, axis=-1, keepdims=True)
    h = x * lax.rsqrt(ms + EPS) * gain + shift
    z = jnp.dot(h.astype(BF16), w_ref[...], preferred_element_type=F32)
    z_ref[0] = z.astype(BF16)


def _in_projection(x, mods_l, norm_g, w_in_bf16):
    tm = IN_TM
    return pl.pallas_call(
        _inproj_kernel,
        grid=(BATCH, SEQ // tm),
        in_specs=[
            pl.BlockSpec((1, tm, D_MODEL), lambda b, i: (b, i, 0)),
            pl.BlockSpec((1, 1, 6 * D_MODEL), lambda b, i: (b, 0, 0)),
            pl.BlockSpec((1, D_MODEL), lambda b, i: (0, 0)),
            pl.BlockSpec((D_MODEL, IN_WIDTH), lambda b, i: (0, 0)),
        ],
        out_specs=pl.BlockSpec((1, tm, IN_WIDTH), lambda b, i: (b, i, 0)),
        out_shape=jax.ShapeDtypeStruct((BATCH, SEQ, IN_WIDTH), BF16),
        compiler_params=_cparams(("arbitrary", "arbitrary"), 48),
        name="in_projection",
    )(x, mods_l.reshape(BATCH, 1, 6 * D_MODEL), norm_g.reshape(1, D_MODEL), w_in_bf16)


def _rms_rows(y, g):
    ms = jnp.mean(y * y, axis=-1, keepdims=True)
    return y * lax.rsqrt(ms + EPS) * g


def _conv_kernel(val_ref, gate_ref, bg_ref, cg_ref, hv_ref, wa_ref, ba_ref, ga_ref, wb_ref, gb_ref,
                 o_ref, upad_ref, vpad_ref):
    zeros = jnp.zeros((CONV_PAD, CONF_WIDTH), F32)
    upad_ref[0:CONV_PAD, :] = zeros
    upad_ref[CONV_PAD + SEQ:CONV_PAD + SEQ + CONV_PAD, :] = zeros
    vpad_ref[0:CONV_PAD, :] = zeros
    vpad_ref[CONV_PAD + SEQ:CONV_PAD + SEQ + CONV_PAD, :] = zeros
    upad_ref[CONV_PAD:CONV_PAD + SEQ, :] = val_ref[0].astype(F32) * jax.nn.sigmoid(gate_ref[0].astype(F32))
    vpad_ref[CONV_PAD:CONV_PAD + SEQ, :] = cg_ref[0].astype(F32) * hv_ref[0].astype(F32)

    rows = CONV_ROWS

    def depthwise(pad_ref, w_ref, ntaps, r0, acc):
        base = CONV_PAD - ntaps // 2
        nwin = rows + 2 * CONV_PAD
        win = pad_ref[pl.ds(r0, nwin), :]
        for r in range(8):
            offs = [base + t for t in range(ntaps) if (base + t) % 8 == r]
            if not offs:
                continue
            shifted = win if r == 0 else pltpu.roll(win, nwin - r, 0)
            for off in offs:
                acc = acc + w_ref[off - base:off - base + 1, :] * shifted[off - r:off - r + rows]
        return acc

    def body(i, carry):
        r0 = pl.multiple_of(i * rows, rows)
        acc = depthwise(upad_ref, wa_ref, CONF_KERNEL, r0, jnp.broadcast_to(ba_ref[...], (rows, CONF_WIDTH)))
        ya = _rms_rows(acc, ga_ref[...])
        ya = ya * jax.nn.sigmoid(ya)
        o_ref[0, pl.ds(r0, rows), 0:CONF_WIDTH] = ya.astype(BF16)

        accb = depthwise(vpad_ref, wb_ref, SC_KERNEL, r0, jnp.zeros((rows, SC_WIDTH), F32))
        yb = bg_ref[0, pl.ds(r0, rows), :].astype(F32) * accb
        o_ref[0, pl.ds(r0, rows), CONF_WIDTH:CONF_WIDTH + SC_WIDTH] = _rms_rows(yb, gb_ref[...]).astype(BF16)
        return carry

    lax.fori_loop(0, SEQ // rows, body, 0)


def _conv_branches(z, conv_a_w, conv_a_b, conv_a_norm_g, conv_b_w, sc_norm_g):
    w = CONF_WIDTH
    zcol = lambda j: pl.BlockSpec((1, SEQ, w), lambda b, j=j: (b, 0, j))
    full2 = lambda shape: pl.BlockSpec(shape, lambda b: (0, 0))
    return pl.pallas_call(
        _conv_kernel,
        grid=(BATCH,),
        in_specs=[
            zcol(0), zcol(1),
            zcol(ZB_OFF // w), zcol(ZB_OFF // w + 1), zcol(ZB_OFF // w + 2),
            full2((CONF_KERNEL, w)), full2((1, w)), full2((1, w)),
            full2((SC_KERNEL, w)), full2((1, w)),
        ],
        out_specs=pl.BlockSpec((1, SEQ, 2 * w), lambda b: (b, 0, 0)),
        out_shape=jax.ShapeDtypeStruct((BATCH, SEQ, 2 * w), BF16),
        scratch_shapes=[
            pltpu.VMEM((SEQ + 2 * CONV_PAD, w), F32),
            pltpu.VMEM((SEQ + 2 * CONV_PAD, w), F32),
        ],
        compiler_params=_cparams(("arbitrary",), 48),
        name="conv_branches",
    )(z, z, z, z, z, conv_a_w, conv_a_b.reshape(1, w), conv_a_norm_g.reshape(1, w),
      conv_b_w, sc_norm_g.reshape(1, w))


def _attn_kernel(zq_ref, zk_ref, zv_ref, cos_ref, sin_ref, qg_ref, kg_ref, lam_ref, og_ref, seg_ref,
                 o_ref, qh_ref, k0_ref, k1_ref, vt_ref, sa_ref, sb_ref, *, lambda_init):
    cos = cos_ref[...]
    sin = sin_ref[...]
    seg = seg_ref[...]
    lane = lax.broadcasted_iota(jnp.int32, (SEQ, V_HEAD_DIM), 1)
    low_quarter = (lane % QK_HEAD_DIM) < (QK_HEAD_DIM // 2)
    in_half0 = lane < QK_HEAD_DIM

    def norm_rope(z_ref, g_ref):
        t = z_ref[0].astype(F32)
        tt = t * t
        tt_hi = tt.astype(BF16)
        tt_lo = (tt - tt_hi.astype(F32)).astype(BF16)
        ms = (jnp.dot(tt_hi, seg, preferred_element_type=F32)
              + jnp.dot(tt_lo, seg, preferred_element_type=F32))
        y = t * lax.rsqrt(ms + EPS) * g_ref[...]
        swapped = jnp.where(low_quarter,
                            pltpu.roll(y, V_HEAD_DIM - QK_HEAD_DIM // 2, 1),
                            pltpu.roll(y, QK_HEAD_DIM // 2, 1))
        return y * cos + swapped * sin

    qh_ref[...] = (norm_rope(zq_ref, qg_ref) * (QK_HEAD_DIM ** -0.5 * LOG2_E)).astype(BF16)
    kh = norm_rope(zk_ref, kg_ref)
    k0_ref[...] = jnp.where(in_half0, kh, 0.0).astype(BF16)
    k1_ref[...] = jnp.where(in_half0, 0.0, kh).astype(BF16)
    vt_ref[...] = zv_ref[0].astype(F32).T.astype(BF16)

    lam4 = lam_ref[...]
    lam = (jnp.exp(jnp.sum(lam4[0:1] * lam4[1:2], axis=-1, keepdims=True))
           - jnp.exp(jnp.sum(lam4[2:3] * lam4[3:4], axis=-1, keepdims=True))
           + lambda_init)
    out_gain = og_ref[...] * (1.0 - lambda_init)

    tq = ATTN_TQ
    kc = ATTN_KC
    nblk = SEQ // tq
    nt = (((1,), (1,)), ((), ()))

    def emit(blk, o0, l0, o1, l1):
        r0 = pl.multiple_of(blk * tq, tq)
        o = o0 / l0 - o1 * (lam / l1)
        ms = jnp.mean(o * o, axis=0, keepdims=True)
        y = o * lax.rsqrt(ms + EPS) * out_gain
        o_ref[0, pl.ds(r0, tq), :] = y.T.astype(BF16)

    gq = jnp.max(jnp.abs(qg_ref[...]), axis=-1, keepdims=True)
    gk = jnp.max(jnp.abs(kg_ref[...]), axis=-1, keepdims=True)
    score_bound = gq * gk * (QK_HEAD_DIM ** 0.5 * LOG2_E)
    bound_is_safe = score_bound[0, 0] <= SAFE_EXP2_SHIFT

    @pl.when(bound_is_safe)
    def _single_pass():
        nchunk = SEQ // kc
        units = [(blk, h, c) for blk in range(nblk) for h in range(2) for c in range(nchunk)]
        k_refs = (k0_ref, k1_ref)

        def qk(u):
            blk, h, c = units[u]
            return lax.dot_general(k_refs[h][c * kc:(c + 1) * kc, :], qh_ref[blk * tq:(blk + 1) * tq, :], nt,
                                   preferred_element_type=F32)

        acc = [None, None]
        den = [None, None]
        pending = [qk(u) for u in range(ATTN_LOOKAHEAD)]
        for u, (blk, h, c) in enumerate(units):
            s = pending.pop(0)
            if u + ATTN_LOOKAHEAD < len(units):
                pending.append(qk(u + ATTN_LOOKAHEAD))
            p = jnp.exp2(s - score_bound)
            d = jnp.sum(p, axis=0, keepdims=True)
            pv = jnp.dot(vt_ref[:, c * kc:(c + 1) * kc], p.astype(BF16), preferred_element_type=F32)
            acc[h] = pv if c == 0 else acc[h] + pv
            den[h] = d if c == 0 else den[h] + d
            if h == 1 and c == nchunk - 1:
                emit(blk, acc[0], den[0], acc[1], den[1])

    @pl.when(jnp.logical_not(bound_is_safe))
    def _row_max_two_pass():
        def scores(blk, s_ref):
            r0 = pl.multiple_of(blk * tq, tq)
            qb = qh_ref[pl.ds(r0, tq), :]
            s_ref[0] = lax.dot_general(k0_ref[...], qb, nt, preferred_element_type=F32)
            s_ref[1] = lax.dot_general(k1_ref[...], qb, nt, preferred_element_type=F32)

        def softmax_pv(s_ref, h):
            s = s_ref[h]
            m = jnp.max(s, axis=0, keepdims=True)
            p = jnp.exp2(s - m)
            den = jnp.sum(p, axis=0, keepdims=True)
            return jnp.dot(vt_ref[...], p.astype(BF16), preferred_element_type=F32), den

        def finish(blk, s_ref):
            o0, l0 = softmax_pv(s_ref, 0)
            o1, l1 = softmax_pv(s_ref, 1)
            emit(blk, o0, l0, o1, l1)

        scores(0, sa_ref)

        def body(j, carry):
            scores(2 * j + 1, sb_ref)
            finish(2 * j, sa_ref)
            scores(2 * j + 2, sa_ref)
            finish(2 * j + 1, sb_ref)
            return carry

        lax.fori_loop(0, nblk // 2 - 1, body, 0)
        scores(nblk - 1, sb_ref)
        finish(nblk - 2, sa_ref)
        finish(nblk - 1, sb_ref)


def _diff_attention(z, cos, sin, q_g, k_g, lam4, out_g, lambda_init):
    hd = V_HEAD_DIM
    seg = np.zeros((hd, hd), np.float32)
    seg[:QK_HEAD_DIM, :QK_HEAD_DIM] = 1.0 / QK_HEAD_DIM
    seg[QK_HEAD_DIM:, QK_HEAD_DIM:] = 1.0 / QK_HEAD_DIM
    zcol = lambda off: pl.BlockSpec((1, SEQ, hd), lambda b, h, off=off: (b, 0, off // hd + h))
    full2 = lambda shape: pl.BlockSpec(shape, lambda b, h: (0, 0))
    return pl.pallas_call(
        functools.partial(_attn_kernel, lambda_init=lambda_init),
        grid=(BATCH, ATTN_HEADS),
        in_specs=[
            zcol(ZQ_OFF), zcol(ZK_OFF), zcol(ZV_OFF),
            full2((SEQ, hd)), full2((SEQ, hd)),
            full2((1, hd)), full2((1, hd)),
            full2((4, QK_HEAD_DIM)), full2((hd, 1)), full2((hd, hd)),
        ],
        out_specs=pl.BlockSpec((1, SEQ, hd), lambda b, h: (b, 0, h)),
        out_shape=jax.ShapeDtypeStruct((BATCH, SEQ, ATTN_WIDTH), BF16),
        scratch_shapes=[
            pltpu.VMEM((SEQ, hd), BF16),
            pltpu.VMEM((SEQ, hd), BF16),
            pltpu.VMEM((SEQ, hd), BF16),
            pltpu.VMEM((hd, SEQ), BF16),
            pltpu.VMEM((2, SEQ, ATTN_TQ), F32),
            pltpu.VMEM((2, SEQ, ATTN_TQ), F32),
        ],
        compiler_params=_cparams(("arbitrary", "arbitrary"), 48),
        name="diff_attention",
    )(z, z, z, cos, sin, jnp.tile(q_g, 2).reshape(1, hd), jnp.tile(k_g, 2).reshape(1, hd),
      lam4, out_g.reshape(hd, 1), jnp.asarray(seg, BF16))


def _ffn_kernel(x_ref, yab_ref, yc_ref, mod_ref, g_ref, wo_ref, wgu_ref, wd_ref, o_ref):
    d = D_MODEL
    gate1 = mod_ref[0, :, 2 * d:3 * d]
    shift2 = mod_ref[0, :, 3 * d:4 * d]
    scale2 = mod_ref[0, :, 4 * d:5 * d]
    gate2 = mod_ref[0, :, 5 * d:6 * d]
    n_ab = CONF_WIDTH + SC_WIDTH
    mix = (jnp.dot(yab_ref[0], wo_ref[0:n_ab, :], preferred_element_type=F32)
           + jnp.dot(yc_ref[0], wo_ref[n_ab:D_MIX, :], preferred_element_type=F32))
    x1 = x_ref[0] + gate1 * mix
    ms = jnp.mean(x1 * x1, axis=-1, keepdims=True)
    h = (x1 * lax.rsqrt(ms + EPS) * (g_ref[...] * (1.0 + scale2)) + shift2).astype(BF16)
    acc = jnp.zeros(x1.shape, F32)
    for j in range(D_FF // FFN_CK):
        c0 = j * FFN_CK
        g = jnp.dot(h, wgu_ref[:, c0:c0 + FFN_CK], preferred_element_type=F32)
        u = jnp.dot(h, wgu_ref[:, D_FF + c0:D_FF + c0 + FFN_CK], preferred_element_type=F32)
        a = (g * jax.nn.sigmoid(g) * u).astype(BF16)
        acc = acc + jnp.dot(a, wd_ref[c0:c0 + FFN_CK, :], preferred_element_type=F32)
    o_ref[0] = x1 + gate2 * acc


def _out_ffn(x, yab, yc, mods_l, norm_g, w_out, w_gate_up, w_down):
    tm = FFN_TM
    resident = lambda shape: pl.BlockSpec(shape, lambda b, i: (0, 0), pipeline_mode=pl.Buffered(1))
    tok = lambda width: pl.BlockSpec((1, tm, width), lambda b, i: (b, i, 0))
    return pl.pallas_call(
        _ffn_kernel,
        grid=(BATCH, SEQ // tm),
        in_specs=[
            tok(D_MODEL), tok(CONF_WIDTH + SC_WIDTH), tok(ATTN_WIDTH),
            pl.BlockSpec((1, 1, 6 * D_MODEL), lambda b, i: (b, 0, 0)),
            pl.BlockSpec((1, D_MODEL), lambda b, i: (0, 0)),
            resident((D_MIX, D_MODEL)), resident((D_MODEL, 2 * D_FF)), resident((D_FF, D_MODEL)),
        ],
        out_specs=tok(D_MODEL),
        out_shape=jax.ShapeDtypeStruct((BATCH, SEQ, D_MODEL), F32),
        compiler_params=_cparams(("arbitrary", "arbitrary"), 56),
        name="out_ffn",
    )(x, yab, yc, mods_l.reshape(BATCH, 1, 6 * D_MODEL), norm_g.reshape(1, D_MODEL),
      w_out, w_gate_up, w_down)


def kernel(x, c, positions, norm1_g, norm2_g, w_ada, b_ada, w_in, conv_a_w, conv_a_b, conv_a_norm_g,
           conv_b_w, sc_norm_g, q_norm_g, k_norm_g, lam_q1, lam_k1, lam_q2, lam_k2, attn_norm_g,
           w_out, w_gate_up, w_down):
    mods = _ada_mods(c, w_ada, b_ada)
    cos, sin = _rope_tables(positions)
    w_in_b = w_in.astype(BF16)
    w_out_b = w_out.astype(BF16)
    w_gu_b = w_gate_up.astype(BF16)
    w_down_b = w_down.astype(BF16)
    for l in range(DEPTH):
        lambda_init = 0.8 - 0.6 * float(np.exp(-0.3 * l))
        z = _in_projection(x, mods[l], norm1_g[l], w_in_b[l])
        yab = _conv_branches(z, conv_a_w[l], conv_a_b[l], conv_a_norm_g[l], conv_b_w[l], sc_norm_g[l])
        lam4 = jnp.stack([lam_q1[l], lam_k1[l], lam_q2[l], lam_k2[l]])
        yc = _diff_attention(z, cos, sin, q_norm_g[l], k_norm_g[l], lam4, attn_norm_g[l], lambda_init)
        x = _out_ffn(x, yab, yc, mods[l], norm2_g[l], w_out_b[l], w_gu_b[l], w_down_b[l])
    return x
```

```python
import functools

import numpy as np
import jax
import jax.numpy as jnp
from jax import lax
from jax.experimental import pallas as pl
from jax.experimental.pallas import tpu as pltpu

D_MODEL = 1024
BATCH = 32
SEQ = 2048
DEPTH = 4
CONF_WIDTH = D_MODEL // 4
CONF_KERNEL = 31
SC_WIDTH = D_MODEL // 4
SC_KERNEL = 3
ATTN_WIDTH = D_MODEL // 2
ATTN_HEADS = 4
V_HEAD_DIM = ATTN_WIDTH // ATTN_HEADS
QK_HEAD_DIM = V_HEAD_DIM // 2
D_MIX = CONF_WIDTH + SC_WIDTH + ATTN_WIDTH
QK_WIDTH = ATTN_HEADS * 2 * QK_HEAD_DIM
IN_WIDTH = 2 * CONF_WIDTH + 3 * SC_WIDTH + 2 * QK_WIDTH + ATTN_WIDTH
D_FF = -(-8 * D_MODEL // (3 * 256)) * 256
ROPE_THETA = 10000.0
EPS = 1e-6
LOG2_E = 1.4426950408889634

ZB_OFF = 2 * CONF_WIDTH
ZQ_OFF = ZB_OFF + 3 * SC_WIDTH
ZK_OFF = ZQ_OFF + QK_WIDTH
ZV_OFF = ZK_OFF + QK_WIDTH

_lane = np.arange(V_HEAD_DIM)
_QK_FEATURE_OF_LANE = (_lane % 32) + 32 * (_lane // 64)
_QK_SOURCE_COLUMN_OF_LANE = ((_lane % 64) // 32) * QK_HEAD_DIM + _QK_FEATURE_OF_LANE


def _qk_layout(w_in):
    cols = np.arange(IN_WIDTH)
    for off in (ZQ_OFF, ZK_OFF):
        for h in range(ATTN_HEADS):
            base = off + h * V_HEAD_DIM
            cols[base:base + V_HEAD_DIM] = base + _QK_SOURCE_COLUMN_OF_LANE
    return jnp.take(w_in, jnp.asarray(cols), axis=2)


V7X_LANES = 128
V7X_VMEM_BYTES = 64 * 1024 * 1024

BF16 = jnp.bfloat16
F32 = jnp.float32

IN_TM = 512
FFN_TM = 512
FFN_CK = 256
FFN_STEPS = BATCH * (SEQ // FFN_TM) + 1
ATTN_TQ = 512
ATTN_KC = 256
ATTN_LOOKAHEAD = 2
SAFE_EXP2_SHIFT = 60.0
CONV_ROWS = 64
CONV_PAD = 16


def _cparams(sem, vmem_mib):
    return pltpu.CompilerParams(dimension_semantics=sem, vmem_limit_bytes=vmem_mib * 1024 * 1024)


def _ada_kernel(c_ref, w_ref, b_ref, o_ref):
    c = c_ref[...]
    c_act = (c * jax.nn.sigmoid(c)).astype(BF16)
    o_ref[0] = jnp.dot(c_act, w_ref[0].astype(BF16), preferred_element_type=F32) + b_ref[0]


def _ada_mods(c, w_ada, b_ada):
    nblk = 6
    return pl.pallas_call(
        _ada_kernel,
        grid=(DEPTH, nblk),
        in_specs=[
            pl.BlockSpec((BATCH, D_MODEL), lambda l, j: (0, 0)),
            pl.BlockSpec((1, D_MODEL, D_MODEL), lambda l, j: (l, 0, j)),
            pl.BlockSpec((1, 1, D_MODEL), lambda l, j: (l, 0, j)),
        ],
        out_specs=pl.BlockSpec((1, BATCH, D_MODEL), lambda l, j: (l, 0, j)),
        out_shape=jax.ShapeDtypeStruct((DEPTH, BATCH, 6 * D_MODEL), F32),
        compiler_params=_cparams(("arbitrary", "arbitrary"), 32),
        name="ada_mods",
    )(c, w_ada, b_ada.reshape(DEPTH, 1, 6 * D_MODEL))


def _rope_kernel(pos_ref, freq_ref, sign_ref, cos_ref, sin_ref):
    ang = pos_ref[...].astype(F32) * freq_ref[...]
    cos_ref[...] = jnp.cos(ang)
    sin_ref[...] = jnp.sin(ang) * sign_ref[...]


def _rope_tables(positions):
    inv_freq = ROPE_THETA ** (-np.arange(0, QK_HEAD_DIM, 2, dtype=np.float32) / QK_HEAD_DIM)
    freq = inv_freq.astype(np.float32)[_QK_FEATURE_OF_LANE % (QK_HEAD_DIM // 2)][None, :]
    sign = np.where(_QK_FEATURE_OF_LANE < QK_HEAD_DIM // 2, -1.0, 1.0).astype(np.float32)[None, :]
    full = lambda: (0, 0)
    return pl.pallas_call(
        _rope_kernel,
        in_specs=[
            pl.BlockSpec((SEQ, 1), full),
            pl.BlockSpec((1, V_HEAD_DIM), full),
            pl.BlockSpec((1, V_HEAD_DIM), full),
        ],
        out_specs=[pl.BlockSpec((SEQ, V_HEAD_DIM), full)] * 2,
        out_shape=[jax.ShapeDtypeStruct((SEQ, V_HEAD_DIM), F32)] * 2,
        name="rope_tables",
    )(positions.reshape(SEQ, 1), jnp.asarray(freq), jnp.asarray(sign))


def _inproj_kernel(x_ref, mod_ref, g_ref, w_ref, cos_ref, sin_ref, qg_ref, kg_ref, seg_ref,
                   zab_ref, qh_ref, k0_ref, k1_ref, vt_ref):
    x = x_ref[0]
    shift = mod_ref[0, :, 0:D_MODEL]
    scale = mod_ref[0, :, D_MODEL:2 * D_MODEL]
    gain = g_ref[...] * (1.0 + scale)
    ms = jnp.mean(x * x, axis=-1, keepdims=True)
    h = (x * lax.rsqrt(ms + EPS) * gain + shift).astype(BF16)

    tm = x.shape[0]
    hd = V_HEAD_DIM
    pair_w = 2 * hd
    cos = cos_ref[...]
    sin = sin_ref[...]
    seg = seg_ref[...]
    lane = lax.broadcasted_iota(jnp.int32, (tm, hd), 1)
    in_half0 = (lane % QK_HEAD_DIM) < (QK_HEAD_DIM // 2)

    n_pairs = QK_WIDTH // pair_w
    proj = [jnp.dot(h, w_ref[:, ZQ_OFF + pair_w * j:ZQ_OFF + pair_w * (j + 1)], preferred_element_type=F32)
            for j in range(2 * n_pairs)]
    msq = [jnp.dot((t * t).astype(BF16), seg, preferred_element_type=F32) for t in proj]
    zv = jnp.dot(h, w_ref[:, ZV_OFF:IN_WIDTH], preferred_element_type=F32)
    zab = jnp.dot(h, w_ref[:, 0:ZQ_OFF], preferred_element_type=F32)

    def rope(ys):
        return ys * cos + pltpu.roll(ys, hd // 2, 1) * sin

    for j in range(2 * n_pairs):
        is_q = j < n_pairs
        g2 = qg_ref[...] if is_q else kg_ref[...]
        y = proj[j] * lax.rsqrt(msq[j] + EPS) * g2
        for s in range(2):
            c0 = ((j % n_pairs) * 2 + s) * hd
            r = rope(y[:, s * hd:(s + 1) * hd])
            if is_q:
                qh_ref[0, :, c0:c0 + hd] = r.astype(BF16)
            else:
                k0_ref[0, :, c0:c0 + hd] = jnp.where(in_half0, r, 0.0).astype(BF16)
                k1_ref[0, :, c0:c0 + hd] = jnp.where(in_half0, 0.0, r).astype(BF16)

    vt_ref[0] = zv.T.astype(BF16)
    zab_ref[0] = zab.astype(BF16)


def _in_projection(x, mods_l, norm_g, w_in_bf16, cos, sin, q_g, k_g):
    tm = IN_TM
    hd = V_HEAD_DIM
    half_of_lane = (np.arange(hd) % QK_HEAD_DIM) // (QK_HEAD_DIM // 2)
    same_half = (half_of_lane[:, None] == half_of_lane[None, :]).astype(np.float32) / QK_HEAD_DIM
    seg = np.kron(np.eye(2, dtype=np.float32), same_half)
    feat = jnp.asarray(np.tile(_QK_FEATURE_OF_LANE, 2))
    qg2 = (q_g * (QK_HEAD_DIM ** -0.5 * LOG2_E))[feat].reshape(1, 2 * hd)
    kg2 = k_g[feat].reshape(1, 2 * hd)
    const = lambda shape: pl.BlockSpec(shape, lambda b, i: (0, 0))
    tok = lambda width: pl.BlockSpec((1, tm, width), lambda b, i: (b, i, 0))
    act = lambda width: jax.ShapeDtypeStruct((BATCH, SEQ, width), BF16)
    return pl.pallas_call(
        _inproj_kernel,
        grid=(BATCH, SEQ // tm),
        in_specs=[
            tok(D_MODEL),
            pl.BlockSpec((1, 1, 6 * D_MODEL), lambda b, i: (b, 0, 0)),
            const((1, D_MODEL)),
            const((D_MODEL, IN_WIDTH)),
            pl.BlockSpec((tm, hd), lambda b, i: (i, 0)),
            pl.BlockSpec((tm, hd), lambda b, i: (i, 0)),
            const((1, 2 * hd)), const((1, 2 * hd)), const((2 * hd, 2 * hd)),
        ],
        out_specs=[tok(ZQ_OFF), tok(QK_WIDTH), tok(QK_WIDTH), tok(QK_WIDTH),
                   pl.BlockSpec((1, ATTN_WIDTH, tm), lambda b, i: (b, 0, i))],
        out_shape=[act(ZQ_OFF), act(QK_WIDTH), act(QK_WIDTH), act(QK_WIDTH),
                   jax.ShapeDtypeStruct((BATCH, ATTN_WIDTH, SEQ), BF16)],
        compiler_params=_cparams(("arbitrary", "arbitrary"), 48),
        name="in_projection",
    )(x, mods_l.reshape(BATCH, 1, 6 * D_MODEL), norm_g.reshape(1, D_MODEL), w_in_bf16, cos, sin,
      qg2, kg2, jnp.asarray(seg, BF16))


def _rms_rows(y, g):
    ms = jnp.mean(y * y, axis=-1, keepdims=True)
    return y * lax.rsqrt(ms + EPS) * g


def _depthwise(pad_ref, w_ref, ntaps, r0, acc):
    rows = CONV_ROWS
    base = CONV_PAD - ntaps // 2
    nwin = rows + 2 * CONV_PAD
    parts = []
    for c0 in range(0, acc.shape[1], V7X_LANES):
        part = acc[:, c0:c0 + V7X_LANES]
        win = pad_ref[r0:r0 + nwin, c0:c0 + V7X_LANES]
        for r in range(8):
            offs = [base + t for t in range(ntaps) if (base + t) % 8 == r]
            if not offs:
                continue
            shifted = win if r == 0 else pltpu.roll(win, nwin - r, 0)
            for off in offs:
                part = part + w_ref[off - base:off - base + 1, c0:c0 + V7X_LANES] * shifted[off - r:off - r + rows]
        parts.append(part)
    return jnp.concatenate(parts, axis=1)


def _attn_kernel(qh_in, k0_in, k1_in, vt_in, qg_ref, kg_ref, lam_ref, og_ref,
                 o_ref, sa_ref, sb_ref, *, lambda_init):
    qh_ref = qh_in.at[0]
    k0_ref = k0_in.at[0]
    k1_ref = k1_in.at[0]
    vt_ref = vt_in.at[0]

    lam4 = lam_ref[...]
    lam = (jnp.exp(jnp.sum(lam4[0:1] * lam4[1:2], axis=-1, keepdims=True))
           - jnp.exp(jnp.sum(lam4[2:3] * lam4[3:4], axis=-1, keepdims=True))
           + lambda_init)
    out_gain = og_ref[...] * (1.0 - lambda_init)

    tq = ATTN_TQ
    kc = ATTN_KC
    nblk = SEQ // tq
    nt = (((1,), (1,)), ((), ()))

    def emit(blk, o0, l0, o1, l1):
        r0 = pl.multiple_of(blk * tq, tq)
        o = o0 / l0 - o1 * (lam / l1)
        ms = jnp.mean(o * o, axis=0, keepdims=True)
        y = o * lax.rsqrt(ms + EPS) * out_gain
        o_ref[0, pl.ds(r0, tq), :] = y.T.astype(BF16)

    gq = jnp.max(jnp.abs(qg_ref[...]), axis=-1, keepdims=True)
    gk = jnp.max(jnp.abs(kg_ref[...]), axis=-1, keepdims=True)
    score_bound = gq * gk * (QK_HEAD_DIM ** 0.5 * LOG2_E)
    bound_is_safe = score_bound[0, 0] <= SAFE_EXP2_SHIFT

    @pl.when(bound_is_safe)
    def _single_pass():
        nchunk = SEQ // kc
        units = [(blk, h, c) for blk in range(nblk) for h in range(2) for c in range(nchunk)]
        k_refs = (k0_ref, k1_ref)

        def qk(u):
            blk, h, c = units[u]
            return lax.dot_general(k_refs[h][c * kc:(c + 1) * kc, :], qh_ref[blk * tq:(blk + 1) * tq, :], nt,
                                   preferred_element_type=F32)

        acc = [None, None]
        den = [None, None]
        pending = [qk(u) for u in range(ATTN_LOOKAHEAD)]
        for u, (blk, h, c) in enumerate(units):
            s = pending.pop(0)
            if u + ATTN_LOOKAHEAD < len(units):
                pending.append(qk(u + ATTN_LOOKAHEAD))
            p = jnp.exp2(s - score_bound)
            d = jnp.sum(p, axis=0, keepdims=True)
            pv = jnp.dot(vt_ref[:, c * kc:(c + 1) * kc], p.astype(BF16), preferred_element_type=F32)
            acc[h] = pv if c == 0 else acc[h] + pv
            den[h] = d if c == 0 else den[h] + d
            if h == 1 and c == nchunk - 1:
                emit(blk, acc[0], den[0], acc[1], den[1])

    @pl.when(jnp.logical_not(bound_is_safe))
    def _row_max_two_pass():
        def scores(blk, s_ref):
            r0 = pl.multiple_of(blk * tq, tq)
            qb = qh_ref[pl.ds(r0, tq), :]
            s_ref[0] = lax.dot_general(k0_ref[...], qb, nt, preferred_element_type=F32)
            s_ref[1] = lax.dot_general(k1_ref[...], qb, nt, preferred_element_type=F32)

        def softmax_pv(s_ref, h):
            s = s_ref[h]
            m = jnp.max(s, axis=0, keepdims=True)
            p = jnp.exp2(s - m)
            den = jnp.sum(p, axis=0, keepdims=True)
            return jnp.dot(vt_ref[...], p.astype(BF16), preferred_element_type=F32), den

        def finish(blk, s_ref):
            o0, l0 = softmax_pv(s_ref, 0)
            o1, l1 = softmax_pv(s_ref, 1)
            emit(blk, o0, l0, o1, l1)

        scores(0, sa_ref)

        def body(j, carry):
            scores(2 * j + 1, sb_ref)
            finish(2 * j, sa_ref)
            scores(2 * j + 2, sa_ref)
            finish(2 * j + 1, sb_ref)
            return carry

        lax.fori_loop(0, nblk // 2 - 1, body, 0)
        scores(nblk - 1, sb_ref)
        finish(nblk - 2, sa_ref)
        finish(nblk - 1, sb_ref)


def _diff_attention(qh, k0, k1, vt, q_g, k_g, lam4, out_g, lambda_init):
    hd = V_HEAD_DIM
    head = pl.BlockSpec((1, SEQ, hd), lambda b, h: (b, 0, h))
    full2 = lambda shape: pl.BlockSpec(shape, lambda b, h: (0, 0))
    return pl.pallas_call(
        functools.partial(_attn_kernel, lambda_init=lambda_init),
        grid=(BATCH, ATTN_HEADS),
        in_specs=[
            head, head, head,
            pl.BlockSpec((1, hd, SEQ), lambda b, h: (b, h, 0)),
            full2((1, QK_HEAD_DIM)), full2((1, QK_HEAD_DIM)),
            full2((4, QK_HEAD_DIM)), full2((hd, 1)),
        ],
        out_specs=head,
        out_shape=jax.ShapeDtypeStruct((BATCH, SEQ, ATTN_WIDTH), BF16),
        scratch_shapes=[
            pltpu.VMEM((2, SEQ, ATTN_TQ), F32),
            pltpu.VMEM((2, SEQ, ATTN_TQ), F32),
        ],
        compiler_params=_cparams(("arbitrary", "arbitrary"), 48),
        name="diff_attention",
    )(qh, k0, k1, vt, q_g.reshape(1, QK_HEAD_DIM), k_g.reshape(1, QK_HEAD_DIM), lam4, out_g.reshape(hd, 1))


def _ffn_kernel(x_ref, yc_ref, mod_ref, g_ref, wo_ref, wgu_ref, wd_ref,
                zab_ref, zprev_ref, znext_ref, wa_ref, ba_ref, ga_ref, wb_ref, gb_ref,
                o_ref, yab_ref, upad_ref, vpad_ref):
    tm = FFN_TM
    d = D_MODEL
    n = pl.program_id(0)

    @pl.when(n == 0)
    def _():
        yab_ref[...] = jnp.zeros(yab_ref.shape, yab_ref.dtype)

    gate1 = mod_ref[0, :, 2 * d:3 * d]
    shift2 = mod_ref[0, :, 3 * d:4 * d]
    scale2 = mod_ref[0, :, 4 * d:5 * d]
    gate2 = mod_ref[0, :, 5 * d:6 * d]
    n_ab = CONF_WIDTH + SC_WIDTH
    mix = (jnp.dot(yab_ref[...], wo_ref[0:n_ab, :], preferred_element_type=F32)
           + jnp.dot(yc_ref[0], wo_ref[n_ab:D_MIX, :], preferred_element_type=F32))
    x1 = x_ref[0] + gate1 * mix
    ms = jnp.mean(x1 * x1, axis=-1, keepdims=True)
    h = (x1 * lax.rsqrt(ms + EPS) * (g_ref[...] * (1.0 + scale2)) + shift2).astype(BF16)

    w = CONF_WIDTH
    tile_in_seq = lax.rem(jnp.minimum(n, FFN_STEPS - 2), SEQ // tm)
    has_prev = jnp.where(tile_in_seq > 0, 1.0, 0.0)
    has_next = jnp.where(tile_in_seq < SEQ // tm - 1, 1.0, 0.0)

    def glu(z):
        return z[:, 0:w].astype(F32) * jax.nn.sigmoid(z[:, w:2 * w].astype(F32))

    def gated(z):
        return z[:, ZB_OFF + w:ZB_OFF + 2 * w].astype(F32) * z[:, ZB_OFF + 2 * w:ZB_OFF + 3 * w].astype(F32)

    zt, zp, zn = zab_ref[0], zprev_ref[0], znext_ref[0]
    upad_ref[0:CONV_PAD, :] = glu(zp) * has_prev
    upad_ref[CONV_PAD:CONV_PAD + tm, :] = glu(zt)
    upad_ref[CONV_PAD + tm:CONV_PAD + tm + CONV_PAD, :] = glu(zn) * has_next
    vpad_ref[0:CONV_PAD, :] = gated(zp) * has_prev
    vpad_ref[CONV_PAD:CONV_PAD + tm, :] = gated(zt)
    vpad_ref[CONV_PAD + tm:CONV_PAD + tm + CONV_PAD, :] = gated(zn) * has_next

    def zero_after(v):
        bits = lax.bitcast_convert_type(v[0:1, :], jnp.uint32)
        return lax.shift_right_logical(lax.shift_right_logical(bits, jnp.uint32(16)), jnp.uint32(16)).astype(F32)

    def conv_rows(r0):
        acc = _depthwise(upad_ref, wa_ref, CONF_KERNEL, r0, jnp.broadcast_to(ba_ref[...], (CONV_ROWS, w)))
        ya = _rms_rows(acc, ga_ref[...])
        ya = ya * jax.nn.sigmoid(ya)
        yab_ref[r0:r0 + CONV_ROWS, 0:w] = ya.astype(BF16)
        accb = _depthwise(vpad_ref, wb_ref, SC_KERNEL, r0, jnp.zeros((CONV_ROWS, w), F32))
        yb = zab_ref[0, r0:r0 + CONV_ROWS, ZB_OFF:ZB_OFF + w].astype(F32) * accb
        yb = _rms_rows(yb, gb_ref[...])
        yab_ref[r0:r0 + CONV_ROWS, w:2 * w] = yb.astype(BF16)
        return zero_after(ya) + zero_after(yb)

    n_ffn = D_FF // FFN_CK
    n_conv = tm // CONV_ROWS
    acc = jnp.zeros(x1.shape, F32)
    done = 0
    for j in range(n_ffn):
        c0 = j * FFN_CK
        g = jnp.dot(h, wgu_ref[:, c0:c0 + FFN_CK], preferred_element_type=F32)
        u = jnp.dot(h, wgu_ref[:, D_FF + c0:D_FF + c0 + FFN_CK], preferred_element_type=F32)
        act = g * jax.nn.sigmoid(g) * u
        upto = (j + 1) * n_conv // n_ffn
        for k in range(done, upto):
            act = act + conv_rows(k * CONV_ROWS)
        done = upto
        acc = acc + jnp.dot(act.astype(BF16), wd_ref[c0:c0 + FFN_CK, :], preferred_element_type=F32)
    o_ref[0] = x1 + gate2 * acc


def _out_ffn(x, zab, yc, mods_l, norm_g, w_out, w_gate_up, w_down,
             conv_a_w, conv_a_b, conv_a_norm_g, conv_b_w, sc_norm_g):
    tm = FFN_TM
    nt = SEQ // tm
    n_tiles = BATCH * nt
    halo_per_tile = tm // CONV_PAD
    w = CONF_WIDTH

    def mm_tile(n):
        m = jnp.maximum(n - 1, 0)
        return m // nt, m % nt

    def conv_tile(n):
        c = jnp.minimum(n, n_tiles - 1)
        return c // nt, c % nt

    def tok(width):
        return pl.BlockSpec((1, tm, width), lambda n: (*mm_tile(n), 0))

    def prev_halo(n):
        b, i = conv_tile(n)
        return b, jnp.maximum(i * halo_per_tile - 1, 0), 0

    def next_halo(n):
        b, i = conv_tile(n)
        return b, jnp.minimum((i + 1) * halo_per_tile, SEQ // CONV_PAD - 1), 0

    resident = lambda shape: pl.BlockSpec(shape, lambda n: (0, 0), pipeline_mode=pl.Buffered(1))
    const = lambda shape: pl.BlockSpec(shape, lambda n: (0, 0))
    return pl.pallas_call(
        _ffn_kernel,
        grid=(FFN_STEPS,),
        in_specs=[
            tok(D_MODEL), tok(ATTN_WIDTH),
            pl.BlockSpec((1, 1, 6 * D_MODEL), lambda n: (mm_tile(n)[0], 0, 0)),
            const((1, D_MODEL)),
            resident((D_MIX, D_MODEL)), resident((D_MODEL, 2 * D_FF)), resident((D_FF, D_MODEL)),
            pl.BlockSpec((1, tm, ZQ_OFF), lambda n: (*conv_tile(n), 0)),
            pl.BlockSpec((1, CONV_PAD, ZQ_OFF), prev_halo),
            pl.BlockSpec((1, CONV_PAD, ZQ_OFF), next_halo),
            const((CONF_KERNEL, w)), const((1, w)), const((1, w)), const((SC_KERNEL, w)), const((1, w)),
        ],
        out_specs=tok(D_MODEL),
        out_shape=jax.ShapeDtypeStruct((BATCH, SEQ, D_MODEL), F32),
        scratch_shapes=[
            pltpu.VMEM((tm, 2 * w), BF16),
            pltpu.VMEM((tm + 2 * CONV_PAD, w), F32),
            pltpu.VMEM((tm + 2 * CONV_PAD, w), F32),
        ],
        compiler_params=_cparams(("arbitrary",), 56),
        name="out_ffn",
    )(x, yc, mods_l.reshape(BATCH, 1, 6 * D_MODEL), norm_g.reshape(1, D_MODEL),
      w_out, w_gate_up, w_down, zab, zab, zab,
      conv_a_w, conv_a_b.reshape(1, w), conv_a_norm_g.reshape(1, w), conv_b_w, sc_norm_g.reshape(1, w))


def kernel(x, c, positions, norm1_g, norm2_g, w_ada, b_ada, w_in, conv_a_w, conv_a_b, conv_a_norm_g,
           conv_b_w, sc_norm_g, q_norm_g, k_norm_g, lam_q1, lam_k1, lam_q2, lam_k2, attn_norm_g,
           w_out, w_gate_up, w_down):
    mods = _ada_mods(c, w_ada, b_ada)
    cos, sin = _rope_tables(positions)
    w_in_b = _qk_layout(w_in.astype(BF16))
    w_out_b = w_out.astype(BF16)
    w_gu_b = w_gate_up.astype(BF16)
    w_down_b = w_down.astype(BF16)
    for l in range(DEPTH):
        lambda_init = 0.8 - 0.6 * float(np.exp(-0.3 * l))
        zab, qh, k0, k1, vt = _in_projection(x, mods[l], norm1_g[l], w_in_b[l], cos, sin,
                                             q_norm_g[l], k_norm_g[l])
        lam4 = jnp.stack([lam_q1[l], lam_k1[l], lam_q2[l], lam_k2[l]])
        yc = _diff_attention(qh, k0, k1, vt, q_norm_g[l], k_norm_g[l], lam4, attn_norm_g[l], lambda_init)
        x = _out_ffn(x, zab, yc, mods[l], norm2_g[l], w_out_b[l], w_gu_b[l], w_down_b[l],
                     conv_a_w[l], conv_a_b[l], conv_a_norm_g[l], conv_b_w[l], sc_norm_g[l])
    return x
```
